```python
import jax, jax.numpy as jnp
from jax import lax
import numpy as np

D_MODEL = 1024
BATCH = 8
SEQ = 4096
DEPTH = 2

MEM_LEN = 256
N_BRANCH = 4
CHUNK = 64
EPS = 1e-6
MLSTM_HEADS = 4
MLSTM_HD = D_MODEL // (N_BRANCH * MLSTM_HEADS)
MLSTM_W = MLSTM_HEADS * MLSTM_HD
CONV_WIDTH = 4
GLA_HEADS = 4
GLA_DK = D_MODEL // (2 * N_BRANCH * GLA_HEADS)
GLA_DV = D_MODEL // (N_BRANCH * GLA_HEADS)
GLA_QK_W = GLA_HEADS * GLA_DK
GLA_V_W = GLA_HEADS * GLA_DV
GLA_RANK = 16
GLA_TAU = 16.0
S5_GROUP = 16
S5_GROUPS = D_MODEL // (N_BRANCH * S5_GROUP)
S5_W = S5_GROUPS * S5_GROUP
S5_STATE = 64
S5_DT_MIN = 0.001
S5_DT_MAX = 0.1
HGRN_HEADS = 4
HGRN_DK = D_MODEL // (N_BRANCH * HGRN_HEADS)
HGRN_W = HGRN_HEADS * HGRN_DK
XATTN_HEADS = 4
XATTN_HD = D_MODEL // XATTN_HEADS
MLP_HIDDEN = 4 * D_MODEL
MIX_SPLIT_WIDTHS = (
    MLSTM_W, MLSTM_W, MLSTM_W, MLSTM_HEADS, MLSTM_HEADS, MLSTM_W,
    GLA_QK_W, GLA_QK_W, GLA_V_W, GLA_RANK, GLA_V_W,
    S5_W,
    HGRN_W, HGRN_W, HGRN_W, HGRN_W,
    N_BRANCH * D_MODEL,
)
IN_WIDTH = sum(MIX_SPLIT_WIDTHS)

kernel_name = 'hybrid_gated_parallel_mixer_block'


def rmsnorm(x, g):
    xf = x.astype(jnp.float32)
    ms = jnp.mean(xf * xf, axis=-1, keepdims=True)
    return (xf * lax.rsqrt(ms + EPS) * g.astype(jnp.float32)).astype(x.dtype)


def head_rmsnorm(y, g):
    y = y * lax.rsqrt(jnp.mean(y * y, axis=-1, keepdims=True) + EPS)
    return y.reshape(y.shape[0], y.shape[1], -1) * g.astype(jnp.float32)


def causal_depthwise_conv(x, w, b):
    k = w.shape[0]
    y = lax.conv_general_dilated(x, w.astype(x.dtype)[:, None, :], window_strides=(1,),
                                 padding=[(k - 1, 0)], dimension_numbers=('NWC', 'WIO', 'NWC'),
                                 feature_group_count=x.shape[-1])
    return y + b.astype(x.dtype)


def to_chunks(t):
    b, s, h, d = t.shape
    return t.reshape(b, s // CHUNK, CHUNK, h, d).transpose(0, 3, 1, 2, 4)


def from_chunks(t):
    b, h, nc, l, d = t.shape
    return t.transpose(0, 2, 3, 1, 4).reshape(b, nc * l, h, d)


def gate_chunks(t):
    b, s, h = t.shape
    return t.reshape(b, s // CHUNK, CHUNK, h).transpose(0, 3, 1, 2)


def mlstm_chunkwise(q, k, v, i_pre, f_pre):
    bsz, _, nh, dh = q.shape
    qc = to_chunks(q) * (dh ** -0.5)
    kc = to_chunks(k)
    vc = to_chunks(v)
    li = gate_chunks(i_pre)
    lf = jax.nn.log_sigmoid(gate_chunks(f_pre))
    b = jnp.cumsum(lf, axis=-1)
    b_end = b[..., -1]
    a = b_end[..., None] - b + li
    m_loc = jnp.max(a, axis=-1)
    w = jnp.exp(a - m_loc[..., None])
    kv_loc = jnp.einsum('bhcsk,bhcsv,bhcs->bhckv', kc, vc, w)
    n_loc = jnp.einsum('bhcsk,bhcs->bhck', kc, w)

    def step(carry, inp):
        c_st, n_st, m_st = carry
        kv_c, n_c, mloc_c, bend_c = inp
        m_new = jnp.maximum(bend_c + m_st, mloc_c)
        s_old = jnp.exp(bend_c + m_st - m_new)
        s_new = jnp.exp(mloc_c - m_new)
        c_new = s_old[..., None, None] * c_st + s_new[..., None, None] * kv_c
        n_new = s_old[..., None] * n_st + s_new[..., None] * n_c
        return (c_new, n_new, m_new), (c_st, n_st, m_st)

    init = (jnp.zeros((bsz, nh, dh, dh), q.dtype), jnp.zeros((bsz, nh, dh), q.dtype),
            jnp.zeros((bsz, nh), q.dtype))
    xs = (jnp.moveaxis(kv_loc, 2, 0), jnp.moveaxis(n_loc, 2, 0),
          jnp.moveaxis(m_loc, 2, 0), jnp.moveaxis(b_end, 2, 0))
    _, (c_in, n_in, m_in) = lax.scan(step, init, xs)
    c_in = jnp.moveaxis(c_in, 0, 2)
    n_in = jnp.moveaxis(n_in, 0, 2)
    m_in = jnp.moveaxis(m_in, 0, 2)

    causal = jnp.tril(jnp.ones((CHUNK, CHUNK), dtype=bool))
    log_d = jnp.where(causal, b[..., :, None] - b[..., None, :] + li[..., None, :], -jnp.inf)
    inter_log = b + m_in[..., None]
    m_t = jnp.maximum(jnp.max(log_d, axis=-1), inter_log)
    scores = jnp.einsum('bhctk,bhcsk->bhcts', qc, kc) * jnp.exp(log_d - m_t[..., None])
    inter = jnp.exp(inter_log - m_t)
    num = (jnp.einsum('bhcts,bhcsv->bhctv', scores, vc)
           + inter[..., None] * jnp.einsum('bhctk,bhckv->bhctv', qc, c_in))
    den = jnp.sum(scores, axis=-1) + inter * jnp.einsum('bhctk,bhck->bhct', qc, n_in)
    h = num / jnp.maximum(jnp.abs(den), jnp.exp(-m_t))[..., None]
    return from_chunks(h)


def chunk_gated_linear(q, k, v, log_g):
    qc, kc, vc, gc = [jnp.moveaxis(to_chunks(t), 2, 0) for t in (q, k, v, log_g)]
    g_cum = jnp.cumsum(gc, axis=-2)
    causal = jnp.tril(jnp.ones((CHUNK, CHUNK), dtype=bool))

    def step(s_st, inp):
        qb, kb, vb, gb = inp
        g_end = gb[:, :, -1, :]
        o_inter = jnp.einsum('bhtk,bhkv->bhtv', qb * jnp.exp(gb), s_st)
        diff = gb[:, :, :, None, :] - gb[:, :, None, :, :]
        decay = jnp.exp(jnp.where(causal[:, :, None], diff, -jnp.inf))
        att = jnp.einsum('bhtk,bhsk,bhtsk->bhts', qb, kb, decay)
        o = o_inter + jnp.einsum('bhts,bhsv->bhtv', att, vb)
        s_new = (jnp.exp(g_end)[..., None] * s_st
                 + jnp.einsum('bhsk,bhsv->bhkv', kb * jnp.exp(g_end[:, :, None, :] - gb), vb))
        return s_new, o

    nc, bsz, nh, _, dk = qc.shape
    s0 = jnp.zeros((bsz, nh, dk, vc.shape[-1]), q.dtype)
    _, o = lax.scan(step, s0, (qc, kc, vc, g_cum))
    return from_chunks(jnp.moveaxis(o, 0, 2))


def _diag_combine(e1, e2):
    a1, b1 = e1
    a2, b2 = e2
    return a1 * a2, a2 * b1 + b2


def s5_ssm(u, lam_re, lam_im, b_re, b_im, c_re, c_im, d_skip, log_step):
    f32 = jnp.float32
    bsz, s, _ = u.shape
    u = u.astype(f32).reshape(bsz, s, S5_GROUPS, S5_GROUP)
    lam = lax.complex(lam_re.astype(f32), lam_im.astype(f32))
    step = jnp.exp(log_step.astype(f32))[:, None]
    lam_bar = jnp.exp(lam * step)
    b_bar = ((lam_bar - 1.0) / lam)[..., None] * lax.complex(b_re.astype(f32), b_im.astype(f32))
    bu = jnp.einsum('bsgh,gph->bsgp', u.astype(jnp.complex64), b_bar)
    a = jnp.broadcast_to(lam_bar, bu.shape)
    _, state = lax.associative_scan(_diag_combine, (a, bu), axis=1)
    c = lax.complex(c_re.astype(f32), c_im.astype(f32))
    y = jnp.real(jnp.einsum('bsgp,ghp->bsgh', state, c)) + d_skip.astype(f32) * u
    return y.reshape(bsz, s, S5_W)


def hybrid_mixer(h, w_in, m_conv_w, m_conv_b, m_i_bias, m_f_bias, m_norm, gla_w_a2, gla_b_a,
                 gla_norm, s5_lam_re, s5_lam_im, s5_b_re, s5_b_im, s5_c_re, s5_c_im, s5_d,
                 s5_log_step, s5_w_glu, s5_b_glu, hg_lb, hg_norm, w_branch, w_out):
    f32 = jnp.float32
    dt = h.dtype
    bsz, s, _ = h.shape
    split_points = [int(i) for i in np.cumsum(MIX_SPLIT_WIDTHS)[:-1]]
    (m_q, m_k, m_v, m_i, m_f, m_o, g_q, g_k, g_v, g_a, g_r, s_u,
     r_q, r_f, r_i, r_g, gate_cols) = jnp.split(h @ w_in, split_points, axis=-1)

    def heads(t, n):
        return t.astype(f32).reshape(bsz, s, n, -1)

    qk = jax.nn.silu(causal_depthwise_conv(jnp.concatenate([m_q, m_k], axis=-1), m_conv_w, m_conv_b))
    mq, mk = jnp.split(qk, 2, axis=-1)
    hm = mlstm_chunkwise(heads(mq, MLSTM_HEADS), heads(mk, MLSTM_HEADS), heads(m_v, MLSTM_HEADS),
                         (m_i + m_i_bias).astype(f32), (m_f + m_f_bias).astype(f32))
    y_a = jax.nn.sigmoid(m_o.astype(f32)) * head_rmsnorm(hm, m_norm)

    log_a = jax.nn.log_sigmoid((g_a @ gla_w_a2 + gla_b_a).astype(f32)) / GLA_TAU
    og = chunk_gated_linear(heads(g_q, GLA_HEADS) * (GLA_DK ** -0.5), heads(g_k, GLA_HEADS),
                            heads(g_v, GLA_HEADS), heads(log_a, GLA_HEADS))
    y_b = jax.nn.silu(g_r.astype(f32)) * head_rmsnorm(og, gla_norm)

    ys = s5_ssm(s_u, s5_lam_re, s5_lam_im, s5_b_re, s5_b_im, s5_c_re, s5_c_im, s5_d, s5_log_step)
    glu_val, glu_gate = jnp.split(jax.nn.gelu(ys) @ s5_w_glu.astype(f32) + s5_b_glu.astype(f32), 2, axis=-1)
    y_c = glu_val * jax.nn.sigmoid(glu_gate)

    z = r_f.astype(f32)
    log_f = jnp.logaddexp(jnp.log(hg_lb), jnp.log1p(-hg_lb) + jax.nn.log_sigmoid(z))
    k_hg = (1.0 - hg_lb) * jax.nn.sigmoid(-z)
    oh = chunk_gated_linear(heads(jax.nn.silu(r_q), HGRN_HEADS), heads(k_hg, HGRN_HEADS),
                            heads(r_i, HGRN_HEADS), heads(log_f, HGRN_HEADS))
    y_d = jax.nn.silu(r_g.astype(f32)) * head_rmsnorm(oh, hg_norm)

    gates = jax.nn.sigmoid(gate_cols.astype(f32)).reshape(bsz, s, N_BRANCH, D_MODEL)
    branch_outs = (y_a, y_b, y_c, y_d)
    merged = jnp.zeros((bsz, s, D_MODEL), f32)
    for n in range(N_BRANCH):
        merged = merged + gates[:, :, n, :] * (branch_outs[n].astype(dt) @ w_branch[n]).astype(f32)
    return merged.astype(dt) @ w_out


def cross_attention(h, mem_n, w_q, w_kv, w_o):
    bsz, s, _ = h.shape
    q = (h @ w_q).reshape(bsz, s, XATTN_HEADS, XATTN_HD)
    k, v = jnp.split(mem_n @ w_kv, 2, axis=-1)
    k = k.reshape(bsz, -1, XATTN_HEADS, XATTN_HD)
    v = v.reshape(bsz, -1, XATTN_HEADS, XATTN_HD)
    scores = jnp.einsum('bshd,bmhd->bhsm', q, k).astype(jnp.float32) * (XATTN_HD ** -0.5)
    p = jax.nn.softmax(scores, axis=-1).astype(v.dtype)
    o = jnp.einsum('bhsm,bmhd->bshd', p, v).reshape(bsz, s, D_MODEL)
    return o @ w_o


def sq_relu_mlp(h, w1, w2):
    return jnp.square(jax.nn.relu(h @ w1)) @ w2


def setup_inputs(seed: int = 0) -> dict:
    key = jax.random.key(seed)
    ks = iter(jax.random.split(key, 40))
    f32 = jnp.float32

    def nrm(shape, scale):
        return jax.random.normal(next(ks), shape, f32) * scale

    def gain(shape):
        return 1.0 + 0.02 * jax.random.normal(next(ks), shape, f32)

    lam_im = (jnp.pi * jnp.arange(S5_STATE, dtype=f32))[None, None, :] + nrm((DEPTH, S5_GROUPS, S5_STATE), 0.01)
    return {
        'x': nrm((BATCH, SEQ, D_MODEL), 1.0),
        'mem': nrm((BATCH, MEM_LEN, D_MODEL), 1.0),
        'norm_mix': gain((DEPTH, D_MODEL)),
        'w_in': nrm((DEPTH, D_MODEL, IN_WIDTH), D_MODEL ** -0.5),
        'mlstm_conv_w': nrm((DEPTH, CONV_WIDTH, 2 * MLSTM_W), CONV_WIDTH ** -0.5),
        'mlstm_conv_b': nrm((DEPTH, 2 * MLSTM_W), 0.01),
        'mlstm_i_bias': nrm((DEPTH, MLSTM_HEADS), 0.1),
        'mlstm_f_bias': jnp.linspace(3.0, 6.0, MLSTM_HEADS, dtype=f32)[None, :] + nrm((DEPTH, MLSTM_HEADS), 0.01),
        'mlstm_norm': gain((DEPTH, MLSTM_W)),
        'gla_w_a2': nrm((DEPTH, GLA_RANK, GLA_QK_W), GLA_RANK ** -0.5),
        'gla_b_a': nrm((DEPTH, GLA_QK_W), 0.01),
        'gla_norm': gain((DEPTH, GLA_V_W)),
        's5_lam_re': -0.5 + nrm((DEPTH, S5_GROUPS, S5_STATE), 0.01),
        's5_lam_im': lam_im,
        's5_b_re': nrm((DEPTH, S5_GROUPS, S5_STATE, S5_GROUP), (2 * S5_GROUP) ** -0.5),
        's5_b_im': nrm((DEPTH, S5_GROUPS, S5_STATE, S5_GROUP), (2 * S5_GROUP) ** -0.5),
        's5_c_re': nrm((DEPTH, S5_GROUPS, S5_GROUP, S5_STATE), S5_STATE ** -0.5),
        's5_c_im': nrm((DEPTH, S5_GROUPS, S5_GROUP, S5_STATE), S5_STATE ** -0.5),
        's5_d': nrm((DEPTH, S5_GROUPS, S5_GROUP), 1.0),
        's5_log_step': jax.random.uniform(next(ks), (DEPTH, S5_GROUPS), f32,
                                          minval=float(np.log(S5_DT_MIN)), maxval=float(np.log(S5_DT_MAX))),
        's5_w_glu': nrm((DEPTH, S5_W, 2 * S5_W), S5_W ** -0.5),
        's5_b_glu': nrm((DEPTH, 2 * S5_W), 0.01),
        'hgrn_lb_logits': nrm((DEPTH, HGRN_W), 0.1),
        'hgrn_norm': gain((DEPTH, HGRN_W)),
        'w_branch': nrm((DEPTH, N_BRANCH, D_MODEL // N_BRANCH, D_MODEL), (D_MODEL // N_BRANCH) ** -0.5),
        'w_out': nrm((DEPTH, D_MODEL, D_MODEL), D_MODEL ** -0.5),
        'norm_xattn': gain((DEPTH, D_MODEL)),
        'norm_mem': gain((DEPTH, D_MODEL)),
        'xattn_w_q': nrm((DEPTH, D_MODEL, D_MODEL), D_MODEL ** -0.5),
        'xattn_w_kv': nrm((DEPTH, D_MODEL, 2 * D_MODEL), D_MODEL ** -0.5),
        'xattn_w_o': nrm((DEPTH, D_MODEL, D_MODEL), D_MODEL ** -0.5),
        'norm_mlp': gain((DEPTH, D_MODEL)),
        'mlp_w1': nrm((DEPTH, D_MODEL, MLP_HIDDEN), D_MODEL ** -0.5),
        'mlp_w2': nrm((DEPTH, MLP_HIDDEN, D_MODEL), MLP_HIDDEN ** -0.5),
        'norm_final': gain((D_MODEL,)),
    }


def reference(x, mem, norm_mix, w_in, mlstm_conv_w, mlstm_conv_b, mlstm_i_bias, mlstm_f_bias,
              mlstm_norm, gla_w_a2, gla_b_a, gla_norm, s5_lam_re, s5_lam_im, s5_b_re, s5_b_im,
              s5_c_re, s5_c_im, s5_d, s5_log_step, s5_w_glu, s5_b_glu, hgrn_lb_logits, hgrn_norm,
              w_branch, w_out, norm_xattn, norm_mem, xattn_w_q, xattn_w_kv, xattn_w_o, norm_mlp,
              mlp_w1, mlp_w2, norm_final):
    lb_all = jnp.cumsum(jax.nn.softmax(hgrn_lb_logits.astype(jnp.float32), axis=0), axis=0)
    lb_all = lb_all - lb_all[:1]
    for l in range(DEPTH):
        h = rmsnorm(x, norm_mix[l])
        x = x + hybrid_mixer(h, w_in[l], mlstm_conv_w[l], mlstm_conv_b[l], mlstm_i_bias[l],
                             mlstm_f_bias[l], mlstm_norm[l], gla_w_a2[l], gla_b_a[l], gla_norm[l],
                             s5_lam_re[l], s5_lam_im[l], s5_b_re[l], s5_b_im[l], s5_c_re[l],
                             s5_c_im[l], s5_d[l], s5_log_step[l], s5_w_glu[l], s5_b_glu[l],
                             lb_all[l], hgrn_norm[l], w_branch[l], w_out[l])
        h = rmsnorm(x, norm_xattn[l])
        x = x + cross_attention(h, rmsnorm(mem, norm_mem[l]), xattn_w_q[l], xattn_w_kv[l], xattn_w_o[l])
        h = rmsnorm(x, norm_mlp[l])
        x = x + sq_relu_mlp(h, mlp_w1[l], mlp_w2[l])
    return rmsnorm(x, norm_final)
```

```python
import functools

import jax
import jax.numpy as jnp
import numpy as np
from jax import lax
from jax.experimental import pallas as pl
from jax.experimental.pallas import tpu as pltpu

F32 = jnp.float32
BF16 = jnp.bfloat16

D_MODEL = 1024
N_BRANCH = 4
CHUNK = 64
EPS = 1e-6
MLSTM_HEADS = 4
MLSTM_W = 256
CONV_WIDTH = 4
GLA_HEADS = 4
GLA_DK = 32
GLA_QK_W = 128
GLA_V_W = 256
GLA_RANK = 16
GLA_TAU = 16.0
S5_GROUP = 16
S5_GROUPS = 16
S5_W = 256
S5_STATE = 64
HGRN_HEADS = 4
HGRN_W = 256
XATTN_HEADS = 4
XATTN_HD = D_MODEL // XATTN_HEADS
MLP_HIDDEN = 4 * D_MODEL
MIX_SPLIT_WIDTHS = (
    MLSTM_W, MLSTM_W, MLSTM_W, MLSTM_HEADS, MLSTM_HEADS, MLSTM_W,
    GLA_QK_W, GLA_QK_W, GLA_V_W, GLA_RANK, GLA_V_W,
    S5_W,
    HGRN_W, HGRN_W, HGRN_W, HGRN_W,
    N_BRANCH * D_MODEL,
)

V7X_VMEM_LIMIT_BYTES = 56 * 1024 * 1024
LANES = 128
ROW_TILE = 512

W_MLSTM = 4 * MLSTM_W + LANES
W_GLA = 2 * GLA_QK_W + 2 * GLA_V_W + LANES
W_S5 = S5_W
W_HGRN = 4 * HGRN_W
IN_GROUP_WIDTHS = (W_MLSTM, W_GLA, W_S5, W_HGRN)


def _compiler_params(n_axes):
    return pltpu.CompilerParams(
        dimension_semantics=("arbitrary",) * n_axes,
        vmem_limit_bytes=V7X_VMEM_LIMIT_BYTES,
    )


def _resident(shape):
    zeros = (0,) * len(shape)
    return pl.BlockSpec(shape, lambda *_: zeros, pipeline_mode=pl.Buffered(1))


def _rms_bf16(x, g):
    ms = jnp.mean(x * x, axis=-1, keepdims=True)
    return (x * lax.rsqrt(ms + EPS) * g).astype(BF16)


def _dot(a, b):
    return jnp.dot(a, b, preferred_element_type=F32)


def _inproj_body(x_ref, g_ref, w_ref, om_ref, og_ref, os_ref, oh_ref):
    h = _rms_bf16(x_ref[...], g_ref[...])
    off = 0
    for o_ref, width in zip((om_ref, og_ref, os_ref, oh_ref), IN_GROUP_WIDTHS):
        o_ref[...] = _dot(h, w_ref[:, off:off + width])
        off += width


def _inproj(x2, g, w_mix):
    n = x2.shape[0]
    return pl.pallas_call(
        _inproj_body,
        grid=(n // ROW_TILE,),
        in_specs=[
            pl.BlockSpec((ROW_TILE, D_MODEL), lambda i: (i, 0)),
            _resident((1, D_MODEL)),
            _resident(w_mix.shape),
        ],
        out_specs=[pl.BlockSpec((ROW_TILE, w), lambda i: (i, 0)) for w in IN_GROUP_WIDTHS],
        out_shape=[jax.ShapeDtypeStruct((n, w), F32) for w in IN_GROUP_WIDTHS],
        compiler_params=_compiler_params(1),
        name="norm_inproj",
    )(x2, g, w_mix)


def _merge_body(x_ref, g_ref, ya_ref, yb_ref, yc_ref, yd_ref, wg_ref, wb_ref, wo_ref, o_ref):
    x = x_ref[...]
    h = _rms_bf16(x, g_ref[...])
    merged = jnp.zeros((ROW_TILE, D_MODEL), F32)
    for n, y_ref in enumerate((ya_ref, yb_ref, yc_ref, yd_ref)):
        gate = jax.nn.sigmoid(_dot(h, wg_ref[:, n * D_MODEL:(n + 1) * D_MODEL]))
        merged = merged + gate * _dot(y_ref[...].astype(BF16), wb_ref[n])
    o_ref[...] = x + _dot(merged.astype(BF16), wo_ref[...])


def _merge(x2, g, ys, w_gate, w_branch, w_out):
    n = x2.shape[0]
    row = lambda w: pl.BlockSpec((ROW_TILE, w), lambda i: (i, 0))
    return pl.pallas_call(
        _merge_body,
        grid=(n // ROW_TILE,),
        in_specs=[row(D_MODEL), _resident((1, D_MODEL))] + [row(D_MODEL // N_BRANCH)] * N_BRANCH
        + [_resident(w_gate.shape), _resident(w_branch.shape), _resident(w_out.shape)],
        out_specs=row(D_MODEL),
        out_shape=jax.ShapeDtypeStruct((n, D_MODEL), F32),
        compiler_params=_compiler_params(1),
        name="branch_merge",
    )(x2, g, *ys, w_gate, w_branch, w_out)


def _kv_body(m_ref, g_ref, w_ref, o_ref):
    o_ref[...] = _dot(_rms_bf16(m_ref[...], g_ref[...]), w_ref[...]).astype(BF16)


def _kv_proj(mem2, g, w_kv):
    n = mem2.shape[0]
    return pl.pallas_call(
        _kv_body,
        grid=(n // ROW_TILE,),
        in_specs=[pl.BlockSpec((ROW_TILE, D_MODEL), lambda i: (i, 0)),
                  _resident((1, D_MODEL)), _resident(w_kv.shape)],
        out_specs=pl.BlockSpec((ROW_TILE, 2 * D_MODEL), lambda i: (i, 0)),
        out_shape=jax.ShapeDtypeStruct((n, 2 * D_MODEL), BF16),
        compiler_params=_compiler_params(1),
        name="mem_kv_proj",
    )(mem2, g, w_kv)


def _xattn_body(x_ref, g_ref, kv_ref, wq_ref, wo_ref, o_ref):
    x = x_ref[...]
    q = _dot(_rms_bf16(x, g_ref[...]), wq_ref[...]).astype(BF16)
    heads = []
    for hd in range(XATTN_HEADS):
        lo = hd * XATTN_HD
        k = kv_ref[:, lo:lo + XATTN_HD]
        v = kv_ref[:, D_MODEL + lo:D_MODEL + lo + XATTN_HD]
        s = lax.dot_general(q[:, lo:lo + XATTN_HD], k, (((1,), (1,)), ((), ())),
                            preferred_element_type=F32) * (XATTN_HD ** -0.5)
        e = jnp.exp(s - jnp.max(s, axis=-1, keepdims=True))
        p = e / jnp.sum(e, axis=-1, keepdims=True)
        heads.append(_dot(p.astype(BF16), v).astype(BF16))
    o = jnp.concatenate(heads, axis=-1)
    o_ref[...] = x + _dot(o, wo_ref[...])


def _xattn(x3, g, kv3, w_q, w_o):
    b, s, _ = x3.shape
    mem_len = kv3.shape[1]
    return pl.pallas_call(
        _xattn_body,
        grid=(b, s // ROW_TILE),
        in_specs=[
            pl.BlockSpec((None, ROW_TILE, D_MODEL), lambda bi, i: (bi, i, 0)),
            _resident((1, D_MODEL)),
            pl.BlockSpec((None, mem_len, 2 * D_MODEL), lambda bi, i: (bi, 0, 0)),
            _resident(w_q.shape), _resident(w_o.shape),
        ],
        out_specs=pl.BlockSpec((None, ROW_TILE, D_MODEL), lambda bi, i: (bi, i, 0)),
        out_shape=jax.ShapeDtypeStruct(x3.shape, F32),
        compiler_params=_compiler_params(2),
        name="cross_attention",
    )(x3, g, kv3, w_q, w_o)


MLP_HIDDEN_CHUNK = 1024


def _mlp_body(x_ref, g_ref, w1_ref, w2_ref, gf_ref, o_ref, *, final_norm):
    x = x_ref[...]
    h = _rms_bf16(x, g_ref[...])
    acc = x
    for j in range(MLP_HIDDEN // MLP_HIDDEN_CHUNK):
        lo = j * MLP_HIDDEN_CHUNK
        a = jnp.maximum(_dot(h, w1_ref[:, lo:lo + MLP_HIDDEN_CHUNK]), 0.0)
        acc = acc + _dot((a * a).astype(BF16), w2_ref[lo:lo + MLP_HIDDEN_CHUNK, :])
    if final_norm:
        ms = jnp.mean(acc * acc, axis=-1, keepdims=True)
        acc = acc * lax.rsqrt(ms + EPS) * gf_ref[...]
    o_ref[...] = acc


def _mlp(x2, g, w1, w2, g_final, final_norm):
    n = x2.shape[0]
    return pl.pallas_call(
        functools.partial(_mlp_body, final_norm=final_norm),
        grid=(n // ROW_TILE,),
        in_specs=[pl.BlockSpec((ROW_TILE, D_MODEL), lambda i: (i, 0)), _resident((1, D_MODEL)),
                  _resident(w1.shape), _resident(w2.shape), _resident((1, D_MODEL))],
        out_specs=pl.BlockSpec((ROW_TILE, D_MODEL), lambda i: (i, 0)),
        out_shape=jax.ShapeDtypeStruct((n, D_MODEL), F32),
        compiler_params=_compiler_params(1),
        name="sq_relu_mlp",
    )(x2, g, w1, w2, g_final)


def _head_rmsnorm(y, g):
    y = y * lax.rsqrt(jnp.mean(y * y, axis=-1, keepdims=True) + EPS)
    return y.reshape(y.shape[0], y.shape[1], -1) * g.astype(F32)


def _causal_depthwise_conv(x, w, b):
    k = w.shape[0]
    y = lax.conv_general_dilated(x, w.astype(x.dtype)[:, None, :], window_strides=(1,),
                                 padding=[(k - 1, 0)], dimension_numbers=('NWC', 'WIO', 'NWC'),
                                 feature_group_count=x.shape[-1])
    return y + b.astype(x.dtype)


def _to_chunks(t):
    b, s, h, d = t.shape
    return t.reshape(b, s // CHUNK, CHUNK, h, d).transpose(0, 3, 1, 2, 4)


def _from_chunks(t):
    b, h, nc, l, d = t.shape
    return t.transpose(0, 2, 3, 1, 4).reshape(b, nc * l, h, d)


def _gate_chunks(t):
    b, s, h = t.shape
    return t.reshape(b, s // CHUNK, CHUNK, h).transpose(0, 3, 1, 2)


def _mlstm_chunkwise(q, k, v, i_pre, f_pre):
    bsz, _, nh, dh = q.shape
    qc = _to_chunks(q) * (dh ** -0.5)
    kc = _to_chunks(k)
    vc = _to_chunks(v)
    li = _gate_chunks(i_pre)
    lf = jax.nn.log_sigmoid(_gate_chunks(f_pre))
    b = jnp.cumsum(lf, axis=-1)
    b_end = b[..., -1]
    a = b_end[..., None] - b + li
    m_loc = jnp.max(a, axis=-1)
    w = jnp.exp(a - m_loc[..., None])
    kv_loc = jnp.einsum('bhcsk,bhcsv,bhcs->bhckv', kc, vc, w)
    n_loc = jnp.einsum('bhcsk,bhcs->bhck', kc, w)

    def step(carry, inp):
        c_st, n_st, m_st = carry
        kv_c, n_c, mloc_c, bend_c = inp
        m_new = jnp.maximum(bend_c + m_st, mloc_c)
        s_old = jnp.exp(bend_c + m_st - m_new)
        s_new = jnp.exp(mloc_c - m_new)
        c_new = s_old[..., None, None] * c_st + s_new[..., None, None] * kv_c
        n_new = s_old[..., None] * n_st + s_new[..., None] * n_c
        return (c_new, n_new, m_new), (c_st, n_st, m_st)

    init = (jnp.zeros((bsz, nh, dh, dh), q.dtype), jnp.zeros((bsz, nh, dh), q.dtype),
            jnp.zeros((bsz, nh), q.dtype))
    xs = (jnp.moveaxis(kv_loc, 2, 0), jnp.moveaxis(n_loc, 2, 0),
          jnp.moveaxis(m_loc, 2, 0), jnp.moveaxis(b_end, 2, 0))
    _, (c_in, n_in, m_in) = lax.scan(step, init, xs)
    c_in = jnp.moveaxis(c_in, 0, 2)
    n_in = jnp.moveaxis(n_in, 0, 2)
    m_in = jnp.moveaxis(m_in, 0, 2)

    causal = jnp.tril(jnp.ones((CHUNK, CHUNK), dtype=bool))
    log_d = jnp.where(causal, b[..., :, None] - b[..., None, :] + li[..., None, :], -jnp.inf)
    inter_log = b + m_in[..., None]
    m_t = jnp.maximum(jnp.max(log_d, axis=-1), inter_log)
    scores = jnp.einsum('bhctk,bhcsk->bhcts', qc, kc) * jnp.exp(log_d - m_t[..., None])
    inter = jnp.exp(inter_log - m_t)
    num = (jnp.einsum('bhcts,bhcsv->bhctv', scores, vc)
           + inter[..., None] * jnp.einsum('bhctk,bhckv->bhctv', qc, c_in))
    den = jnp.sum(scores, axis=-1) + inter * jnp.einsum('bhctk,bhck->bhct', qc, n_in)
    h = num / jnp.maximum(jnp.abs(den), jnp.exp(-m_t))[..., None]
    return _from_chunks(h)


def _chunk_gated_linear(q, k, v, log_g):
    qc, kc, vc, gc = [jnp.moveaxis(_to_chunks(t), 2, 0) for t in (q, k, v, log_g)]
    g_cum = jnp.cumsum(gc, axis=-2)
    causal = jnp.tril(jnp.ones((CHUNK, CHUNK), dtype=bool))

    def step(s_st, inp):
        qb, kb, vb, gb = inp
        g_end = gb[:, :, -1, :]
        o_inter = jnp.einsum('bhtk,bhkv->bhtv', qb * jnp.exp(gb), s_st)
        diff = gb[:, :, :, None, :] - gb[:, :, None, :, :]
        decay = jnp.exp(jnp.where(causal[:, :, None], diff, -jnp.inf))
        att = jnp.einsum('bhtk,bhsk,bhtsk->bhts', qb, kb, decay)
        o = o_inter + jnp.einsum('bhts,bhsv->bhtv', att, vb)
        s_new = (jnp.exp(g_end)[..., None] * s_st
                 + jnp.einsum('bhsk,bhsv->bhkv', kb * jnp.exp(g_end[:, :, None, :] - gb), vb))
        return s_new, o

    nc, bsz, nh, _, dk = qc.shape
    s0 = jnp.zeros((bsz, nh, dk, vc.shape[-1]), q.dtype)
    _, o = lax.scan(step, s0, (qc, kc, vc, g_cum))
    return _from_chunks(jnp.moveaxis(o, 0, 2))


def _diag_combine(e1, e2):
    a1, b1 = e1
    a2, b2 = e2
    return a1 * a2, a2 * b1 + b2


def _s5_ssm(u, lam_re, lam_im, b_re, b_im, c_re, c_im, d_skip, log_step):
    bsz, s, _ = u.shape
    u = u.astype(F32).reshape(bsz, s, S5_GROUPS, S5_GROUP)
    lam = lax.complex(lam_re.astype(F32), lam_im.astype(F32))
    step = jnp.exp(log_step.astype(F32))[:, None]
    lam_bar = jnp.exp(lam * step)
    b_bar = ((lam_bar - 1.0) / lam)[..., None] * lax.complex(b_re.astype(F32), b_im.astype(F32))
    bu = jnp.einsum('bsgh,gph->bsgp', u.astype(jnp.complex64), b_bar)
    a = jnp.broadcast_to(lam_bar, bu.shape)
    _, state = lax.associative_scan(_diag_combine, (a, bu), axis=1)
    c = lax.complex(c_re.astype(F32), c_im.astype(F32))
    y = jnp.real(jnp.einsum('bsgp,ghp->bsgh', state, c)) + d_skip.astype(F32) * u
    return y.reshape(bsz, s, S5_W)


def _mixers(pm, pg, ps, ph, bsz, s, m_conv_w, m_conv_b, m_i_bias, m_f_bias, m_norm, gla_w_a2,
            gla_b_a, gla_norm, s5_lam_re, s5_lam_im, s5_b_re, s5_b_im, s5_c_re, s5_c_im, s5_d,
            s5_log_step, s5_w_glu, s5_b_glu, hg_lb, hg_norm):
    def r3(t):
        return t.reshape(bsz, s, t.shape[-1])

    pm, pg, ps, ph = r3(pm), r3(pg), r3(ps), r3(ph)

    def heads(t, n):
        return t.reshape(bsz, s, n, -1)

    m_q, m_k, m_v, m_o = [pm[..., i * MLSTM_W:(i + 1) * MLSTM_W] for i in range(4)]
    m_i = pm[..., 4 * MLSTM_W:4 * MLSTM_W + MLSTM_HEADS]
    m_f = pm[..., 4 * MLSTM_W + MLSTM_HEADS:4 * MLSTM_W + 2 * MLSTM_HEADS]
    qk = jax.nn.silu(_causal_depthwise_conv(jnp.concatenate([m_q, m_k], axis=-1), m_conv_w, m_conv_b))
    mq, mk = jnp.split(qk, 2, axis=-1)
    hm = _mlstm_chunkwise(heads(mq, MLSTM_HEADS), heads(mk, MLSTM_HEADS), heads(m_v, MLSTM_HEADS),
                          m_i + m_i_bias, m_f + m_f_bias)
    y_a = jax.nn.sigmoid(m_o) * _head_rmsnorm(hm, m_norm)

    g_q = pg[..., :GLA_QK_W]
    g_k = pg[..., GLA_QK_W:2 * GLA_QK_W]
    g_v = pg[..., 2 * GLA_QK_W:2 * GLA_QK_W + GLA_V_W]
    g_r = pg[..., 2 * GLA_QK_W + GLA_V_W:2 * GLA_QK_W + 2 * GLA_V_W]
    g_a = pg[..., 2 * GLA_QK_W + 2 * GLA_V_W:2 * GLA_QK_W + 2 * GLA_V_W + GLA_RANK]
    log_a = jax.nn.log_sigmoid(g_a @ gla_w_a2 + gla_b_a) / GLA_TAU
    og = _chunk_gated_linear(heads(g_q, GLA_HEADS) * (GLA_DK ** -0.5), heads(g_k, GLA_HEADS),
                             heads(g_v, GLA_HEADS), heads(log_a, GLA_HEADS))
    y_b = jax.nn.silu(g_r) * _head_rmsnorm(og, gla_norm)

    ys = _s5_ssm(ps, s5_lam_re, s5_lam_im, s5_b_re, s5_b_im, s5_c_re, s5_c_im, s5_d, s5_log_step)
    glu_val, glu_gate = jnp.split(jax.nn.gelu(ys) @ s5_w_glu + s5_b_glu, 2, axis=-1)
    y_c = glu_val * jax.nn.sigmoid(glu_gate)

    r_q, r_f, r_i, r_g = [ph[..., i * HGRN_W:(i + 1) * HGRN_W] for i in range(4)]
    log_f = jnp.logaddexp(jnp.log(hg_lb), jnp.log1p(-hg_lb) + jax.nn.log_sigmoid(r_f))
    k_hg = (1.0 - hg_lb) * jax.nn.sigmoid(-r_f)
    oh = _chunk_gated_linear(heads(jax.nn.silu(r_q), HGRN_HEADS), heads(k_hg, HGRN_HEADS),
                             heads(r_i, HGRN_HEADS), heads(log_f, HGRN_HEADS))
    y_d = jax.nn.silu(r_g) * _head_rmsnorm(oh, hg_norm)
    return [t.reshape(bsz * s, -1) for t in (y_a, y_b, y_c, y_d)]


def _split_w_in(w_in):
    pts = [int(i) for i in np.cumsum(MIX_SPLIT_WIDTHS)[:-1]]
    (m_q, m_k, m_v, m_i, m_f, m_o, g_q, g_k, g_v, g_a, g_r, s_u,
     r_q, r_f, r_i, r_g, gates) = jnp.split(w_in, pts, axis=-1)

    def pad(t):
        return jnp.pad(t, ((0, 0), (0, LANES - t.shape[-1])))

    w_mix = jnp.concatenate([m_q, m_k, m_v, m_o, pad(jnp.concatenate([m_i, m_f], axis=-1)),
                             g_q, g_k, g_v, g_r, pad(g_a), s_u, r_q, r_f, r_i, r_g], axis=-1)
    return w_mix.astype(BF16), gates.astype(BF16)


def kernel(x, mem, norm_mix, w_in, mlstm_conv_w, mlstm_conv_b, mlstm_i_bias, mlstm_f_bias,
           mlstm_norm, gla_w_a2, gla_b_a, gla_norm, s5_lam_re, s5_lam_im, s5_b_re, s5_b_im,
           s5_c_re, s5_c_im, s5_d, s5_log_step, s5_w_glu, s5_b_glu, hgrn_lb_logits, hgrn_norm,
           w_branch, w_out, norm_xattn, norm_mem, xattn_w_q, xattn_w_kv, xattn_w_o, norm_mlp,
           mlp_w1, mlp_w2, norm_final):
    bsz, s, d = x.shape
    depth = w_in.shape[0]
    mem_len = mem.shape[1]
    lb_all = jnp.cumsum(jax.nn.softmax(hgrn_lb_logits.astype(F32), axis=0), axis=0)
    lb_all = lb_all - lb_all[:1]
    x2 = x.reshape(bsz * s, d)
    mem2 = mem.reshape(bsz * mem_len, d)
    row = lambda t: t.reshape(1, -1)
    for l in range(depth):
        w_mix, w_gate = _split_w_in(w_in[l])
        pm, pg, ps, ph = _inproj(x2, row(norm_mix[l]), w_mix)
        ys = _mixers(pm, pg, ps, ph, bsz, s, mlstm_conv_w[l], mlstm_conv_b[l], mlstm_i_bias[l],
                     mlstm_f_bias[l], mlstm_norm[l], gla_w_a2[l], gla_b_a[l], gla_norm[l],
                     s5_lam_re[l], s5_lam_im[l], s5_b_re[l], s5_b_im[l], s5_c_re[l], s5_c_im[l],
                     s5_d[l], s5_log_step[l], s5_w_glu[l], s5_b_glu[l], lb_all[l], hgrn_norm[l])
        x2 = _merge(x2, row(norm_mix[l]), ys, w_gate, w_branch[l].astype(BF16),
                    w_out[l].astype(BF16))
        kv = _kv_proj(mem2, row(norm_mem[l]), xattn_w_kv[l].astype(BF16))
        x2 = _xattn(x2.reshape(bsz, s, d), row(norm_xattn[l]), kv.reshape(bsz, mem_len, 2 * d),
                    xattn_w_q[l].astype(BF16), xattn_w_o[l].astype(BF16)).reshape(bsz * s, d)
        x2 = _mlp(x2, row(norm_mlp[l]), mlp_w1[l].astype(BF16), mlp_w2[l].astype(BF16),
                  row(norm_final), final_norm=(l == depth - 1))
    return x2.reshape(bsz, s, d)
```
